```python
import jax, jax.numpy as jnp
from jax import lax
import numpy as np

D_MODEL = 1024
BATCH = 8
SEQ = 2048
DEPTH = 2
DEC_BATCH = 128
DEC_SEQ = 1
PAST_LEN = 16384
PAGE_SIZE = 128

N_EVEN = (DEPTH + 1) // 2
N_ODD = DEPTH // 2
D_POOL = D_MODEL // 2
POOL_WINDOWS = (2, 4, 8, 16)
N_POOL_GROUPS = len(POOL_WINDOWS)
POOL_GROUP_DIM = D_POOL // N_POOL_GROUPS
POOL_HIST = max(POOL_WINDOWS) - 1
D_CONV = D_MODEL // 2
CONV_WIDTH = 3
CONV_HIST = CONV_WIDTH - 1
D_IN_EVEN = D_POOL + 3 * D_CONV
D_SGU = D_MODEL
CHUNK = 128
N_SGU_HEADS = 8
SGU_HEAD_DIM = D_SGU // N_SGU_HEADS
N_GROUPS = 4
EXPERTS_PER_GROUP = 4
N_EXPERTS = N_GROUPS * EXPERTS_PER_GROUP
TOP_K = 2
D_EXPERT = 512
EPS = 1e-6

kernel_name = "hybrid_pool_conv_sgu_hmoe_step"


def rmsnorm(x, g):
    xf = x.astype(jnp.float32)
    xf = xf * lax.rsqrt(jnp.mean(xf * xf, axis=-1, keepdims=True) + EPS)
    return (xf * g.astype(jnp.float32)).astype(x.dtype)


def pool_mixer(u, hist, w_pool, pool_scale):
    B, L, _ = u.shape
    ext = u if hist is None else jnp.concatenate([hist.astype(u.dtype), u], axis=1)
    n_prev = ext.shape[1] - L
    cs = jnp.cumsum(ext.astype(jnp.float32), axis=1)
    cs = jnp.pad(cs, ((0, 0), (1, 0), (0, 0)))
    hi = jnp.arange(n_prev + 1, n_prev + L + 1)
    cs_hi = cs[:, hi]
    outs = []
    for g, w in enumerate(POOL_WINDOWS):
        sl = slice(g * POOL_GROUP_DIM, (g + 1) * POOL_GROUP_DIM)
        lo = jnp.maximum(hi - w, 0)
        cnt = (hi - lo).astype(jnp.float32)[None, :, None]
        mean = (cs_hi[:, :, sl] - cs[:, lo, sl]) / cnt
        outs.append(mean - u[:, :, sl].astype(jnp.float32))
    d = jnp.concatenate(outs, axis=-1).astype(u.dtype).reshape(B, L, N_POOL_GROUPS, POOL_GROUP_DIM)
    y = jnp.einsum('blgc,gcd->blgd', d, w_pool).reshape(B, L, D_POOL)
    return y * pool_scale, ext[:, -POOL_HIST:]


def conv_mixer(b_gate, c_gate, x_b, hist, conv_w):
    z = c_gate * x_b
    L = z.shape[1]
    ext = jnp.concatenate([hist.astype(z.dtype), z], axis=1)
    y = conv_w[0] * ext[:, 0:L]
    for k in range(1, CONV_WIDTH):
        y = y + conv_w[k] * ext[:, k:k + L]
    return b_gate * y, ext[:, -CONV_HIST:]


def even_mixer(h, hist_pool, hist_conv, w_in, w_pool, pool_scale, conv_w, w_out):
    p = h @ w_in
    u_a = p[..., :D_POOL]
    b_g = p[..., D_POOL:D_POOL + D_CONV]
    c_g = p[..., D_POOL + D_CONV:D_POOL + 2 * D_CONV]
    x_b = p[..., D_POOL + 2 * D_CONV:]
    y_a, new_pool = pool_mixer(u_a, hist_pool, w_pool, pool_scale)
    y_b, new_conv = conv_mixer(b_g, c_g, x_b, hist_conv, conv_w)
    y = jnp.concatenate([y_a, y_b], axis=-1) @ w_out
    return y, new_pool, new_conv


def sgu_mixer(h, w_in, g_v, w_s, b_s, w_out):
    p = jax.nn.gelu(h @ w_in)
    u = p[..., :D_SGU]
    v = rmsnorm(p[..., D_SGU:], g_v)
    B, L, _ = v.shape
    c = min(L, CHUNK)
    n = L // c
    mask = jnp.tril(jnp.ones((c, c), dtype=w_s.dtype))
    w = w_s[:, :c, :c] * mask
    vh = v.reshape(B, n, c, N_SGU_HEADS, SGU_HEAD_DIM)
    mixed = jnp.einsum('hts,bnshd->bnthd', w, vh) + b_s[:, :c].T[None, None, :, :, None]
    y = (u * mixed.reshape(B, L, D_SGU)) @ w_out
    return y, v[:, -c:]


def hier_moe(h, w_rg, b_rg, w_re, b_re, w1, w3, w2):
    B, L, D = h.shape
    t = h.reshape(B * L, D)
    tf = t.astype(jnp.float32)
    pg = jax.nn.softmax(tf @ w_rg.astype(jnp.float32) + b_rg.astype(jnp.float32), axis=-1)
    pg_top, gi = lax.top_k(pg, 1)
    le = (tf @ w_re.astype(jnp.float32) + b_re.astype(jnp.float32)).reshape(-1, N_GROUPS, EXPERTS_PER_GROUP)
    le_sel = jnp.take_along_axis(le, gi[:, :, None], axis=1)[:, 0]
    pe = jax.nn.softmax(le_sel, axis=-1)
    pe_top, ei = lax.top_k(pe, TOP_K)
    pe_top = pe_top / jnp.sum(pe_top, axis=-1, keepdims=True)
    w_tok = pg_top * pe_top
    e_glob = gi * EXPERTS_PER_GROUP + ei
    gate = jnp.sum(jax.nn.one_hot(e_glob, N_EXPERTS, dtype=jnp.float32) * w_tok[..., None], axis=1)
    hid = jax.nn.silu(jnp.einsum('td,edf->tef', t, w1)) * jnp.einsum('td,edf->tef', t, w3)
    hid = hid * gate.astype(t.dtype)[:, :, None]
    y = jnp.einsum('tef,efd->td', hid, w2)
    return y.reshape(B, L, D)


def setup_inputs(seed: int = 0) -> dict:
    key = jax.random.key(seed)
    ks = jax.random.split(key, 24)
    f32 = jnp.float32

    def nrm(k, shape, scale):
        return jax.random.normal(k, shape, f32) * scale

    return {
        "x_prompt": nrm(ks[0], (BATCH, SEQ, D_MODEL), 1.0),
        "x_sample": nrm(ks[1], (DEC_BATCH, DEC_SEQ, D_MODEL), 1.0),
        "state_pool": nrm(ks[2], (N_EVEN, DEC_BATCH, POOL_HIST, D_POOL), 1.0),
        "state_conv": nrm(ks[3], (N_EVEN, DEC_BATCH, CONV_HIST, D_CONV), 1.0),
        "norm_mix": 1.0 + nrm(ks[4], (DEPTH, D_MODEL), 0.05),
        "w_in_even": nrm(ks[5], (N_EVEN, D_MODEL, D_IN_EVEN), D_MODEL ** -0.5),
        "w_pool": nrm(ks[6], (N_EVEN, N_POOL_GROUPS, POOL_GROUP_DIM, POOL_GROUP_DIM), POOL_GROUP_DIM ** -0.5),
        "pool_scale": 1.0 + nrm(ks[7], (N_EVEN, D_POOL), 0.1),
        "conv_w": nrm(ks[8], (N_EVEN, CONV_WIDTH, D_CONV), CONV_WIDTH ** -0.5),
        "w_out_even": nrm(ks[9], (N_EVEN, D_POOL + D_CONV, D_MODEL), (D_POOL + D_CONV) ** -0.5),
        "w_in_odd": nrm(ks[10], (N_ODD, D_MODEL, 2 * D_SGU), D_MODEL ** -0.5),
        "sgu_norm": 1.0 + nrm(ks[11], (N_ODD, D_SGU), 0.05),
        "w_sgu": nrm(ks[12], (N_ODD, N_SGU_HEADS, CHUNK, CHUNK), CHUNK ** -0.5),
        "b_sgu": 1.0 + nrm(ks[13], (N_ODD, N_SGU_HEADS, CHUNK), 0.1),
        "w_out_odd": nrm(ks[14], (N_ODD, D_SGU, D_MODEL), D_SGU ** -0.5),
        "norm_ffn": 1.0 + nrm(ks[15], (DEPTH, D_MODEL), 0.05),
        "w_router_group": nrm(ks[16], (DEPTH, D_MODEL, N_GROUPS), D_MODEL ** -0.5),
        "b_router_group": nrm(ks[17], (DEPTH, N_GROUPS), 0.01),
        "w_router_expert": nrm(ks[18], (DEPTH, D_MODEL, N_EXPERTS), D_MODEL ** -0.5),
        "b_router_expert": nrm(ks[19], (DEPTH, N_EXPERTS), 0.01),
        "w_gate": nrm(ks[20], (DEPTH, N_EXPERTS, D_MODEL, D_EXPERT), D_MODEL ** -0.5),
        "w_up": nrm(ks[21], (DEPTH, N_EXPERTS, D_MODEL, D_EXPERT), D_MODEL ** -0.5),
        "w_down": nrm(ks[22], (DEPTH, N_EXPERTS, D_EXPERT, D_MODEL), D_EXPERT ** -0.5),
        "norm_final": 1.0 + nrm(ks[23], (D_MODEL,), 0.05),
    }


def reference(x_prompt, x_sample, state_pool, state_conv, norm_mix, w_in_even, w_pool, pool_scale,
              conv_w, w_out_even, w_in_odd, sgu_norm, w_sgu, b_sgu, w_out_odd, norm_ffn,
              w_router_group, b_router_group, w_router_expert, b_router_expert,
              w_gate, w_up, w_down, norm_final):

    def run(x, pool_hists, conv_hists):
        new_pool, new_conv, new_v = [], [], []
        for i in range(DEPTH):
            j = i // 2
            h = rmsnorm(x, norm_mix[i])
            if i % 2 == 0:
                y, np_, nc_ = even_mixer(h, pool_hists[j], conv_hists[j], w_in_even[j], w_pool[j],
                                         pool_scale[j], conv_w[j], w_out_even[j])
                new_pool.append(np_)
                new_conv.append(nc_)
            else:
                y, nv_ = sgu_mixer(h, w_in_odd[j], sgu_norm[j], w_sgu[j], b_sgu[j], w_out_odd[j])
                new_v.append(nv_)
            x = x + y
            x = x + hier_moe(rmsnorm(x, norm_ffn[i]), w_router_group[i], b_router_group[i],
                             w_router_expert[i], b_router_expert[i], w_gate[i], w_up[i], w_down[i])
        return rmsnorm(x, norm_final), jnp.stack(new_pool), jnp.stack(new_conv), jnp.stack(new_v)

    zero_conv = jnp.zeros((BATCH, CONV_HIST, D_CONV), x_prompt.dtype)
    y_prompt, pool_p, conv_p, v_p = run(x_prompt, [None] * N_EVEN, [zero_conv] * N_EVEN)
    y_sample, pool_s, conv_s, v_s = run(x_sample, [state_pool[j] for j in range(N_EVEN)],
                                        [state_conv[j] for j in range(N_EVEN)])
    return (y_prompt, y_sample, pool_p, pool_s, conv_p, conv_s, v_p, v_s)
```

```python
import functools

import jax
import jax.numpy as jnp
from jax import lax
from jax.experimental import pallas as pl
from jax.experimental.pallas import tpu as pltpu

F32 = jnp.float32
BF16 = jnp.bfloat16
I32 = jnp.int32

D_MODEL = 1024
BATCH = 8
SEQ = 2048
DEC_BATCH = 128
T_PROMPT = BATCH * SEQ
T_SAMPLE = DEC_BATCH
T_ALL = T_PROMPT + T_SAMPLE

D_POOL = 512
POOL_WINDOWS = (2, 4, 8, 16)
POOL_GROUP_DIM = 128
POOL_HIST = 15
D_CONV = 512
CONV_HIST = 2
D_SGU = 1024
CHUNK = 128
N_SGU_HEADS = 8
N_GROUPS = 4
EXPERTS_PER_GROUP = 4
N_EXPERTS = 16
D_EXPERT = 512
EPS = 1e-6

LANES = 128
ROUTE_W = LANES
ROW_W = D_MODEL + ROUTE_W
N_PAIRS = 6
N_CLASSES = N_GROUPS * N_PAIRS

TM_MIX = 256
TM_MOE = 384
N_MOE_TILES = T_ALL // TM_MOE
N_VISITS = N_MOE_TILES + N_CLASSES - 1
HALO_U = 16
HALO_Z = 8
N_MIX_STEPS = T_PROMPT // TM_MIX
STEPS_PER_SEQ = SEQ // TM_MIX
T_ROWS = (N_MIX_STEPS + 1) * TM_MIX
VMEM_LIMIT = 56 * 1024 * 1024

assert T_ALL % TM_MOE == 0 and SEQ % TM_MIX == 0 and TM_MIX % CHUNK == 0


def _rms(x, g):
    ms = jnp.mean(x * x, axis=-1, keepdims=True)
    return x * lax.rsqrt(ms + EPS) * g


def _dot(a, b):
    return jnp.dot(a, b, preferred_element_type=F32)


def _route_tile(x, g_ffn, w_r, b_r):
    t = _rms(x, g_ffn)
    t_hi = t.astype(BF16)
    t_lo = (t - t_hi.astype(F32)).astype(BF16)
    w_hi = w_r.astype(BF16)
    w_lo = (w_r - w_hi.astype(F32)).astype(BF16)
    logits = _dot(t_hi, w_hi) + _dot(t_lo, w_hi) + _dot(t_hi, w_lo) + b_r
    lane = lax.broadcasted_iota(I32, logits.shape, 1)
    neg = jnp.float32(-jnp.inf)
    big = jnp.int32(1 << 20)

    lg = jnp.where(lane < N_GROUPS, logits, neg)
    mg = jnp.max(lg, axis=1, keepdims=True)
    sg = jnp.sum(jnp.exp(lg - mg), axis=1, keepdims=True)
    pg_top = 1.0 / sg
    gi = jnp.min(jnp.where(lg == mg, lane, big), axis=1, keepdims=True)

    e_lane = lane - N_GROUPS
    in_grp = (lane >= N_GROUPS) & (lane < N_GROUPS + N_EXPERTS) & ((e_lane >> 2) == gi)
    le = jnp.where(in_grp, logits, neg)
    m1 = jnp.max(le, axis=1, keepdims=True)
    i1 = jnp.min(jnp.where(le == m1, lane, big), axis=1, keepdims=True)
    le2 = jnp.where(lane == i1, neg, le)
    m2 = jnp.max(le2, axis=1, keepdims=True)
    i2 = jnp.min(jnp.where(le2 == m2, lane, big), axis=1, keepdims=True)
    r = jnp.exp(m2 - m1)
    w_top1 = pg_top / (1.0 + r)
    w_top2 = pg_top * r / (1.0 + r)
    e1 = i1 - N_GROUPS - gi * EXPERTS_PER_GROUP
    e2 = i2 - N_GROUPS - gi * EXPERTS_PER_GROUP
    lo = jnp.minimum(e1, e2)
    hi = jnp.maximum(e1, e2)
    g_lo = jnp.where(e1 < e2, w_top1, w_top2)
    g_hi = jnp.where(e1 < e2, w_top2, w_top1)
    pair = jnp.where(lo == 0, 0, jnp.where(lo == 1, 3, 5)) + (hi - lo - 1)
    cls = (gi * N_PAIRS + pair).astype(F32)
    return jnp.where(lane == 0, g_lo, jnp.where(lane == 1, g_hi, jnp.where(lane == 2, cls, 0.0)))


def _pool_project(d, w_pool_ref, pool_scale):
    ys = []
    for g in range(len(POOL_WINDOWS)):
        sl = slice(g * POOL_GROUP_DIM, (g + 1) * POOL_GROUP_DIM)
        ys.append(_dot(d[:, sl].astype(BF16), w_pool_ref[g]))
    return jnp.concatenate(ys, axis=1) * pool_scale


def _append_sample_rows(row_ref, srow_ref):
    row_ref[:T_SAMPLE, :] = srow_ref[...]
    row_ref[T_SAMPLE:, :] = jnp.zeros((row_ref.shape[0] - T_SAMPLE, ROW_W), F32)


def _even_prompt_kernel(x_ref, srow_ref, nm_ref, win_ref, wpool_ref, pscale_ref, convw_ref, wout_ref,
                        nf_ref, wr_ref, br_ref,
                        row_ref, pool_ref, conv_ref, uhalo, zhalo):
    i = pl.program_id(0)

    @pl.when(i < N_MIX_STEPS)
    def _():
        _even_prompt_step(i % STEPS_PER_SEQ, x_ref, nm_ref, win_ref, wpool_ref, pscale_ref, convw_ref,
                          wout_ref, nf_ref, wr_ref, br_ref, row_ref, pool_ref, conv_ref, uhalo, zhalo)

    @pl.when(i == N_MIX_STEPS)
    def _():
        _append_sample_rows(row_ref, srow_ref)


def _even_prompt_step(s, x_ref, nm_ref, win_ref, wpool_ref, pscale_ref, convw_ref, wout_ref,
                      nf_ref, wr_ref, br_ref, row_ref, pool_ref, conv_ref, uhalo, zhalo):
    tm = x_ref.shape[0]

    @pl.when(s == 0)
    def _():
        uhalo[...] = jnp.zeros_like(uhalo)
        zhalo[...] = jnp.zeros_like(zhalo)

    x = x_ref[...]
    h = _rms(x, nm_ref[...]).astype(BF16)
    p = _dot(h, win_ref[...])
    u = p[:, :D_POOL]
    b_g = p[:, D_POOL:D_POOL + D_CONV]
    c_g = p[:, D_POOL + D_CONV:D_POOL + 2 * D_CONV]
    x_b = p[:, D_POOL + 2 * D_CONV:]

    ext = jnp.concatenate([uhalo[...], u], axis=0)
    s2 = ext + pltpu.roll(ext, 1, axis=0)
    s4 = s2[:, 128:] + pltpu.roll(s2[:, 128:], 2, axis=0)
    s8 = s4[:, 128:] + pltpu.roll(s4[:, 128:], 4, axis=0)
    s16 = s8[:, 128:] + pltpu.roll(s8[:, 128:], 8, axis=0)
    wsum = (s2[HALO_U:, :128], s4[HALO_U:, :128], s8[HALO_U:, :128], s16[HALO_U:, :])
    pos1 = (s * tm + 1 + lax.broadcasted_iota(I32, (tm, 1), 0)).astype(F32)
    ds = []
    for g, w in enumerate(POOL_WINDOWS):
        cnt = jnp.minimum(pos1, jnp.float32(w))
        ds.append(wsum[g] / cnt - u[:, g * POOL_GROUP_DIM:(g + 1) * POOL_GROUP_DIM])
    y_a = _pool_project(jnp.concatenate(ds, axis=1), wpool_ref, pscale_ref[...])
    uhalo[...] = u[tm - HALO_U:, :]

    z = c_g * x_b
    zext = jnp.concatenate([zhalo[...], z], axis=0)
    cw = convw_ref[...]
    yc = cw[0:1, :] * pltpu.roll(zext, 2, axis=0) + cw[1:2, :] * pltpu.roll(zext, 1, axis=0) + cw[2:3, :] * zext
    y_b = b_g * yc[HALO_Z:, :]
    zhalo[...] = z[tm - HALO_Z:, :]

    ycat = jnp.concatenate([y_a, y_b], axis=1).astype(BF16)
    x1 = x + _dot(ycat, wout_ref[...])
    row_ref[:, :D_MODEL] = x1
    row_ref[:, D_MODEL:] = _route_tile(x1, nf_ref[...], wr_ref[...], br_ref[...])

    @pl.when(s == STEPS_PER_SEQ - 1)
    def _():
        pool_ref[...] = u[tm - HALO_U:, :]
        conv_ref[...] = z[tm - HALO_Z:, :]


def _full(shape):
    n = len(shape)
    return pl.BlockSpec(shape, lambda *_: (0,) * n)


def _prompt_step(i):
    return jnp.minimum(i, N_MIX_STEPS - 1)


def _even_prompt(x_prompt, srows, nm, win, wpool, pscale, convw, wout, nf, wr, br):
    tm = TM_MIX
    return pl.pallas_call(
        _even_prompt_kernel,
        grid=(N_MIX_STEPS + 1,),
        in_specs=[
            pl.BlockSpec((tm, D_MODEL), lambda i: (_prompt_step(i), 0)),
            _full((T_SAMPLE, ROW_W)),
            _full((1, D_MODEL)), _full((D_MODEL, 2048)), _full((4, 128, 128)), _full((1, D_POOL)),
            _full((3, D_CONV)), _full((D_MODEL, D_MODEL)), _full((1, D_MODEL)),
            _full((D_MODEL, ROUTE_W)), _full((1, ROUTE_W)),
        ],
        out_specs=[
            pl.BlockSpec((tm, ROW_W), lambda i: (i, 0)),
            pl.BlockSpec((None, HALO_U, D_POOL), lambda i: (_prompt_step(i) // STEPS_PER_SEQ, 0, 0)),
            pl.BlockSpec((None, HALO_Z, D_CONV), lambda i: (_prompt_step(i) // STEPS_PER_SEQ, 0, 0)),
        ],
        out_shape=[
            jax.ShapeDtypeStruct((T_ROWS, ROW_W), F32),
            jax.ShapeDtypeStruct((BATCH, HALO_U, D_POOL), F32),
            jax.ShapeDtypeStruct((BATCH, HALO_Z, D_CONV), F32),
        ],
        scratch_shapes=[pltpu.VMEM((HALO_U, D_POOL), F32), pltpu.VMEM((HALO_Z, D_CONV), F32)],
        compiler_params=pltpu.CompilerParams(
            dimension_semantics=("arbitrary",), vmem_limit_bytes=VMEM_LIMIT),
        name="even_mixer_prompt",
    )(x_prompt.reshape(T_PROMPT, D_MODEL), srows, nm, win, wpool, pscale, convw, wout, nf, wr, br)


def _even_sample_kernel(x_ref, sp_ref, sc_ref, nm_ref, win_ref, wpool_ref, pscale_ref,
                        convw_ref, wout_ref, nf_ref, wr_ref, br_ref,
                        row_ref, pool_ref, conv_ref):
    x = x_ref[...]
    h = _rms(x, nm_ref[...]).astype(BF16)
    p = _dot(h, win_ref[...])
    u = p[:, :D_POOL]
    b_g = p[:, D_POOL:D_POOL + D_CONV]
    c_g = p[:, D_POOL + D_CONV:D_POOL + 2 * D_CONV]
    x_b = p[:, D_POOL + 2 * D_CONV:]

    ds = []
    for g, w in enumerate(POOL_WINDOWS):
        sl = slice(g * POOL_GROUP_DIM, (g + 1) * POOL_GROUP_DIM)
        acc = u[:, sl]
        for k in range(1, w):
            acc = acc + sp_ref[:, POOL_HIST - k, sl]
        ds.append(acc / jnp.float32(w) - u[:, sl])
    y_a = _pool_project(jnp.concatenate(ds, axis=1), wpool_ref, pscale_ref[...])
    for k in range(POOL_HIST - 1):
        pool_ref[:, k, :] = sp_ref[:, k + 1, :]
    pool_ref[:, POOL_HIST - 1, :] = u

    z = c_g * x_b
    cw = convw_ref[...]
    h0 = sc_ref[:, 0, :]
    h1 = sc_ref[:, 1, :]
    y_b = b_g * (cw[0:1, :] * h0 + cw[1:2, :] * h1 + cw[2:3, :] * z)
    conv_ref[:, 0, :] = h1
    conv_ref[:, 1, :] = z

    ycat = jnp.concatenate([y_a, y_b], axis=1).astype(BF16)
    x1 = x + _dot(ycat, wout_ref[...])
    row_ref[:, :D_MODEL] = x1
    row_ref[:, D_MODEL:] = _route_tile(x1, nf_ref[...], wr_ref[...], br_ref[...])


def _even_sample(x_sample, sp, sc, nm, win, wpool, pscale, convw, wout, nf, wr, br):
    return pl.pallas_call(
        _even_sample_kernel,
        grid=(1,),
        in_specs=[
            _full((T_SAMPLE, D_MODEL)), _full((T_SAMPLE, POOL_HIST, D_POOL)),
            _full((T_SAMPLE, CONV_HIST, D_CONV)),
            _full((1, D_MODEL)), _full((D_MODEL, 2048)), _full((4, 128, 128)), _full((1, D_POOL)),
            _full((3, D_CONV)), _full((D_MODEL, D_MODEL)), _full((1, D_MODEL)),
            _full((D_MODEL, ROUTE_W)), _full((1, ROUTE_W)),
        ],
        out_specs=[
            _full((T_SAMPLE, ROW_W)),
            _full((T_SAMPLE, POOL_HIST, D_POOL)),
            _full((T_SAMPLE, CONV_HIST, D_CONV)),
        ],
        out_shape=[
            jax.ShapeDtypeStruct((T_SAMPLE, ROW_W), F32),
            jax.ShapeDtypeStruct((T_SAMPLE, POOL_HIST, D_POOL), F32),
            jax.ShapeDtypeStruct((T_SAMPLE, CONV_HIST, D_CONV), F32),
        ],
        compiler_params=pltpu.CompilerParams(
            dimension_semantics=("arbitrary",), vmem_limit_bytes=VMEM_LIMIT),
        name="even_mixer_sample",
    )(x_sample, sp, sc, nm, win, wpool, pscale, convw, wout, nf, wr, br)


def _sgu_front(x, nm, win_ref, gv):
    h = _rms(x, nm).astype(BF16)
    p = jax.nn.gelu(_dot(h, win_ref[...]))
    return p[:, :D_SGU], _rms(p[:, D_SGU:], gv)


def _odd_prompt_kernel(x_ref, srow_ref, nm_ref, win_ref, gv_ref, ws_ref, bst_ref, wout_ref,
                       nf_ref, wr_ref, br_ref, row_ref, v_ref):
    i = pl.program_id(0)

    @pl.when(i < N_MIX_STEPS)
    def _():
        _odd_prompt_step(i % STEPS_PER_SEQ, x_ref, nm_ref, win_ref, gv_ref, ws_ref, bst_ref, wout_ref,
                         nf_ref, wr_ref, br_ref, row_ref, v_ref)

    @pl.when(i == N_MIX_STEPS)
    def _():
        _append_sample_rows(row_ref, srow_ref)


def _odd_prompt_step(s, x_ref, nm_ref, win_ref, gv_ref, ws_ref, bst_ref, wout_ref, nf_ref, wr_ref, br_ref,
                     row_ref, v_ref):
    tm = x_ref.shape[0]
    n_ch = tm // CHUNK
    x = x_ref[...]
    u, v = _sgu_front(x, nm_ref[...], win_ref, gv_ref[...])
    vb = v.astype(BF16)

    ri = lax.broadcasted_iota(I32, (CHUNK, CHUNK), 0)
    ci = lax.broadcasted_iota(I32, (CHUNK, CHUNK), 1)
    tril = ci <= ri
    bst = bst_ref[...]
    mixed_heads = []
    for hd in range(N_SGU_HEADS):
        cs = slice(hd * CHUNK, (hd + 1) * CHUNK)
        w_h = jnp.where(tril, ws_ref[hd], 0.0).astype(BF16)
        rhs = jnp.concatenate([vb[c * CHUNK:(c + 1) * CHUNK, cs] for c in range(n_ch)], axis=1)
        mixed_heads.append(_dot(w_h, rhs) + bst[:, hd:hd + 1])
    mixed = jnp.concatenate(
        [jnp.concatenate([mh[:, c * CHUNK:(c + 1) * CHUNK] for mh in mixed_heads], axis=1)
         for c in range(n_ch)], axis=0)
    x3 = x + _dot((u * mixed).astype(BF16), wout_ref[...])
    row_ref[:, :D_MODEL] = x3
    row_ref[:, D_MODEL:] = _route_tile(x3, nf_ref[...], wr_ref[...], br_ref[...])

    @pl.when(s == STEPS_PER_SEQ - 1)
    def _():
        v_ref[...] = v[tm - CHUNK:, :]


def _odd_prompt(x2, srows, nm, win, gv, ws, bst, wout, nf, wr, br):
    tm = TM_MIX
    return pl.pallas_call(
        _odd_prompt_kernel,
        grid=(N_MIX_STEPS + 1,),
        in_specs=[
            pl.BlockSpec((tm, D_MODEL), lambda i: (_prompt_step(i), 0)),
            _full((T_SAMPLE, ROW_W)),
            _full((1, D_MODEL)), _full((D_MODEL, 2 * D_SGU)), _full((1, D_SGU)),
            _full((N_SGU_HEADS, CHUNK, CHUNK)), _full((CHUNK, N_SGU_HEADS)),
            _full((D_SGU, D_MODEL)), _full((1, D_MODEL)), _full((D_MODEL, ROUTE_W)), _full((1, ROUTE_W)),
        ],
        out_specs=[
            pl.BlockSpec((tm, ROW_W), lambda i: (i, 0)),
            pl.BlockSpec((None, CHUNK, D_SGU), lambda i: (_prompt_step(i) // STEPS_PER_SEQ, 0, 0)),
        ],
        out_shape=[
            jax.ShapeDtypeStruct((T_ROWS, ROW_W), F32),
            jax.ShapeDtypeStruct((BATCH, CHUNK, D_SGU), F32),
        ],
        compiler_params=pltpu.CompilerParams(
            dimension_semantics=("arbitrary",), vmem_limit_bytes=VMEM_LIMIT),
        name="odd_mixer_prompt",
    )(x2, srows, nm, win, gv, ws, bst, wout, nf, wr, br)


def _odd_sample_kernel(x_ref, nm_ref, win_ref, gv_ref, wvec_ref, bvec_ref, wout_ref,
                       nf_ref, wr_ref, br_ref, row_ref, v_ref):
    x = x_ref[...]
    u, v = _sgu_front(x, nm_ref[...], win_ref, gv_ref[...])
    mixed = wvec_ref[...].astype(BF16).astype(F32) * v.astype(BF16).astype(F32) + bvec_ref[...]
    x3 = x + _dot((u * mixed).astype(BF16), wout_ref[...])
    row_ref[:, :D_MODEL] = x3
    row_ref[:, D_MODEL:] = _route_tile(x3, nf_ref[...], wr_ref[...], br_ref[...])
    v_ref[...] = v


def _odd_sample(x2, nm, win, gv, wvec, bvec, wout, nf, wr, br):
    blk = T_PROMPT // T_SAMPLE
    return pl.pallas_call(
        _odd_sample_kernel,
        grid=(1,),
        in_specs=[
            pl.BlockSpec((T_SAMPLE, D_MODEL), lambda i: (blk, 0)),
            _full((1, D_MODEL)), _full((D_MODEL, 2 * D_SGU)), _full((1, D_SGU)),
            _full((1, D_SGU)), _full((1, D_SGU)),
            _full((D_SGU, D_MODEL)), _full((1, D_MODEL)), _full((D_MODEL, ROUTE_W)), _full((1, ROUTE_W)),
        ],
        out_specs=[
            _full((T_SAMPLE, ROW_W)),
            _full((T_SAMPLE, D_SGU)),
        ],
        out_shape=[
            jax.ShapeDtypeStruct((T_SAMPLE, ROW_W), F32),
            jax.ShapeDtypeStruct((T_SAMPLE, D_SGU), F32),
        ],
        compiler_params=pltpu.CompilerParams(
            dimension_semantics=("arbitrary",), vmem_limit_bytes=VMEM_LIMIT),
        name="odd_mixer_sample",
    )(x2, nm, win, gv, wvec, bvec, wout, nf, wr, br)


def _moe_plan(cls):
    tok = jnp.arange(T_ALL, dtype=I32)
    src = jnp.sort(cls * 32768 + tok) & 32767
    counts = jnp.sum((cls[:, None] == jnp.arange(N_CLASSES, dtype=I32)[None, :]).astype(I32), axis=0)
    ends = jnp.cumsum(counts)
    starts = ends - counts
    first_tile = starts // TM_MOE
    last_tile = jnp.maximum(ends - 1, 0) // TM_MOE
    n_vis_c = jnp.where(counts > 0, last_tile - first_tile + 1, 0)
    v_end = jnp.cumsum(n_vis_c)
    v_start = v_end - n_vis_c
    n_vis = v_end[-1]
    v = jnp.arange(N_VISITS, dtype=I32)
    vv = jnp.minimum(v, n_vis - 1)
    c_of_v = jnp.sum((v_end[None, :] <= vv[:, None]).astype(I32), axis=1)
    tile_of_v = first_tile[c_of_v] + (vv - v_start[c_of_v])
    valid = v < n_vis
    prev_tile = jnp.concatenate([jnp.full((1,), -1, I32), tile_of_v[:-1]])
    next_tile = jnp.concatenate([tile_of_v[1:], jnp.full((1,), -1, I32)])
    first = valid & (tile_of_v != prev_tile)
    last = valid & ((tile_of_v != next_tile) | (v == n_vis - 1))
    flags = first.astype(I32) + 2 * last.astype(I32) + 4 * valid.astype(I32)
    grp = c_of_v // N_PAIRS
    pair = c_of_v % N_PAIRS
    lo = jnp.array([0, 0, 0, 1, 1, 2], I32)[pair]
    hi = jnp.array([1, 2, 3, 2, 3, 3], I32)[pair]
    e_a = grp * EXPERTS_PER_GROUP + lo
    e_b = grp * EXPERTS_PER_GROUP + hi
    return src, tile_of_v.astype(I32), c_of_v.astype(I32), e_a.astype(I32), e_b.astype(I32), flags


def _moe_kernel(final_norm, src_ref, vtile_ref, vcls_ref, ea_ref, eb_ref, flag_ref,
                rows_hbm, nf_ref, w1a_ref, w1b_ref, w3a_ref, w3b_ref, w2a_ref, w2b_ref, *rest):
    if final_norm:
        gfin_ref, yp_hbm, ys_hbm, rowbuf, obuf, ybuf, gsem, osem = rest
    else:
        out_hbm, rowbuf, obuf, ybuf, gsem, osem = rest
    v = pl.program_id(0)
    flags = flag_ref[v]
    is_first = (flags & 1) != 0
    is_last = (flags & 2) != 0
    is_valid = (flags & 4) != 0
    tile = vtile_ref[v]
    slot = tile % 2
    n_tiles = N_MOE_TILES

    def gather_start(tl, sl):
        base = tl * TM_MOE

        def body(r, carry):
            tok = src_ref[base + r]
            pltpu.make_async_copy(rows_hbm.at[pl.ds(tok, 1), :], rowbuf.at[sl, pl.ds(r, 1), :],
                                  gsem.at[sl]).start()
            return carry
        lax.fori_loop(0, TM_MOE, body, 0, unroll=8)

    def gather_wait(sl):
        pltpu.make_async_copy(rows_hbm.at[pl.ds(0, TM_MOE), :], rowbuf.at[sl], gsem.at[sl]).wait()

    def scatter_start(tl, sl):
        base = tl * TM_MOE

        def body(r, carry):
            tok = src_ref[base + r]
            srow = obuf.at[sl, pl.ds(r, 1), :]
            if final_norm:
                @pl.when(tok < T_PROMPT)
                def _():
                    pltpu.make_async_copy(srow, yp_hbm.at[pl.ds(tok, 1), :], osem.at[sl]).start()

                @pl.when(tok >= T_PROMPT)
                def _():
                    pltpu.make_async_copy(srow, ys_hbm.at[pl.ds(tok - T_PROMPT, 1), :], osem.at[sl]).start()
            else:
                pltpu.make_async_copy(srow, out_hbm.at[pl.ds(tok, 1), :], osem.at[sl]).start()
            return carry
        lax.fori_loop(0, TM_MOE, body, 0, unroll=8)

    def scatter_wait(sl):
        dst = yp_hbm if final_norm else out_hbm
        pltpu.make_async_copy(obuf.at[sl], dst.at[pl.ds(0, TM_MOE), :], osem.at[sl]).wait()

    @pl.when(v == 0)
    def _():
        gather_start(0, 0)

    @pl.when(is_first)
    def _():
        gather_wait(slot)

        @pl.when(tile + 1 < n_tiles)
        def _():
            gather_start(tile + 1, 1 - slot)

    @pl.when(is_valid)
    def _():
        xb = rowbuf[slot]
        x1 = xb[:, :D_MODEL]
        route = xb[:, D_MODEL:]
        mine = route[:, 2:3] == vcls_ref[v].astype(F32)
        g_lo = jnp.where(mine, route[:, 0:1], 0.0)
        g_hi = jnp.where(mine, route[:, 1:2], 0.0)
        t = _rms(x1, nf_ref[...]).astype(BF16)
        h_a = (jax.nn.silu(_dot(t, w1a_ref[...])) * _dot(t, w3a_ref[...]) * g_lo).astype(BF16)
        h_b = (jax.nn.silu(_dot(t, w1b_ref[...])) * _dot(t, w3b_ref[...]) * g_hi).astype(BF16)
        y = _dot(h_a, w2a_ref[...]) + _dot(h_b, w2b_ref[...])

        @pl.when(is_first)
        def _():
            ybuf[...] = y

        @pl.when(jnp.logical_not(is_first))
        def _():
            ybuf[...] += y

        @pl.when(is_last)
        def _():
            @pl.when(tile >= 2)
            def _():
                scatter_wait(slot)
            out = x1 + ybuf[...]
            if final_norm:
                out = _rms(out, gfin_ref[...])
            obuf[slot] = out
            scatter_start(tile, slot)

            @pl.when(tile == n_tiles - 1)
            def _():
                scatter_wait(1 - slot)
                scatter_wait(slot)


def _moe(rows, plan, nf, w1, w3, w2, g_final=None):
    final_norm = g_final is not None
    src, v_tile, v_cls, e_a, e_b, flags = plan
    wspec_a13 = pl.BlockSpec((None, D_MODEL, D_EXPERT), lambda v, s, t, c, a, b, f: (a[v], 0, 0))
    wspec_b13 = pl.BlockSpec((None, D_MODEL, D_EXPERT), lambda v, s, t, c, a, b, f: (b[v], 0, 0))
    wspec_a2 = pl.BlockSpec((None, D_EXPERT, D_MODEL), lambda v, s, t, c, a, b, f: (a[v], 0, 0))
    wspec_b2 = pl.BlockSpec((None, D_EXPERT, D_MODEL), lambda v, s, t, c, a, b, f: (b[v], 0, 0))
    gspec = pl.BlockSpec((1, D_MODEL), lambda v, *_: (0, 0))
    in_specs = [pl.BlockSpec(memory_space=pl.ANY), gspec,
                wspec_a13, wspec_b13, wspec_a13, wspec_b13, wspec_a2, wspec_b2]
    args = [rows, nf, w1, w1, w3, w3, w2, w2]
    if final_norm:
        in_specs.append(gspec)
        args.append(g_final)
        out_specs = [pl.BlockSpec(memory_space=pl.ANY), pl.BlockSpec(memory_space=pl.ANY)]
        out_shape = [jax.ShapeDtypeStruct((T_PROMPT, D_MODEL), F32),
                     jax.ShapeDtypeStruct((T_SAMPLE, D_MODEL), F32)]
    else:
        out_specs = pl.BlockSpec(memory_space=pl.ANY)
        out_shape = jax.ShapeDtypeStruct((T_ALL, D_MODEL), F32)
    return pl.pallas_call(
        functools.partial(_moe_kernel, final_norm),
        grid_spec=pltpu.PrefetchScalarGridSpec(
            num_scalar_prefetch=6,
            grid=(N_VISITS,),
            in_specs=in_specs,
            out_specs=out_specs,
            scratch_shapes=[
                pltpu.VMEM((2, TM_MOE, ROW_W), F32),
                pltpu.VMEM((2, TM_MOE, D_MODEL), F32),
                pltpu.VMEM((TM_MOE, D_MODEL), F32),
                pltpu.SemaphoreType.DMA((2,)),
                pltpu.SemaphoreType.DMA((2,)),
            ],
        ),
        out_shape=out_shape,
        compiler_params=pltpu.CompilerParams(
            dimension_semantics=("arbitrary",), vmem_limit_bytes=VMEM_LIMIT),
        name="moe_final" if final_norm else "moe",
    )(src, v_tile, v_cls, e_a, e_b, flags, *args)


def _router_weights(w_rg, b_rg, w_re, b_re):
    pad = ROUTE_W - N_GROUPS - N_EXPERTS
    w = jnp.concatenate([w_rg, w_re, jnp.zeros((D_MODEL, pad), F32)], axis=1)
    b = jnp.concatenate([b_rg, b_re, jnp.zeros((pad,), F32)])[None, :]
    return w, b


def kernel(x_prompt, x_sample, state_pool, state_conv, norm_mix, w_in_even, w_pool, pool_scale, conv_w, w_out_even, w_in_odd, sgu_norm, w_sgu, b_sgu, w_out_odd, norm_ffn, w_router_group, b_router_group, w_router_expert, b_router_expert, w_gate, w_up, w_down, norm_final):
    w1 = w_gate.astype(BF16)
    w3 = w_up.astype(BF16)
    w2 = w_down.astype(BF16)
    wr0, br0 = _router_weights(w_router_group[0], b_router_group[0], w_router_expert[0], b_router_expert[0])
    wr1, br1 = _router_weights(w_router_group[1], b_router_group[1], w_router_expert[1], b_router_expert[1])

    even_w = (norm_mix[0][None, :], w_in_even[0].astype(BF16), w_pool[0].astype(BF16), pool_scale[0][None, :],
              conv_w[0], w_out_even[0].astype(BF16), norm_ffn[0][None, :], wr0, br0)
    srows, pool_s, conv_s = _even_sample(x_sample.reshape(T_SAMPLE, D_MODEL), state_pool[0], state_conv[0],
                                         *even_w)
    rows, pool_p16, conv_p8 = _even_prompt(x_prompt, srows, *even_w)
    plan = _moe_plan(rows[:T_ALL, D_MODEL + 2].astype(I32))
    x2 = _moe(rows, plan, norm_ffn[0][None, :], w1[0], w3[0], w2[0])

    odd_w = (norm_mix[1][None, :], w_in_odd[0].astype(BF16), sgu_norm[0][None, :])
    tail_w = (w_out_odd[0].astype(BF16), norm_ffn[1][None, :], wr1, br1)
    wvec = jnp.repeat(w_sgu[0][:, 0, 0], CHUNK)[None, :]
    bvec = jnp.repeat(b_sgu[0][:, 0], CHUNK)[None, :]
    srows, v_s = _odd_sample(x2, *odd_w, wvec, bvec, *tail_w)
    rows, v_p = _odd_prompt(x2, srows, *odd_w, w_sgu[0], b_sgu[0].T, *tail_w)
    plan = _moe_plan(rows[:T_ALL, D_MODEL + 2].astype(I32))
    y_p, y_s = _moe(rows, plan, norm_ffn[1][None, :], w1[1], w3[1], w2[1], g_final=norm_final[None, :])

    return (y_p.reshape(BATCH, SEQ, D_MODEL),
            y_s.reshape(DEC_BATCH, 1, D_MODEL),
            pool_p16[None, :, HALO_U - POOL_HIST:, :],
            pool_s[None],
            conv_p8[None, :, HALO_Z - CONV_HIST:, :],
            conv_s[None],
            v_p[None],
            v_s.reshape(1, DEC_BATCH, 1, D_SGU))
```
